```python
import math, functools
import jax, jax.numpy as jnp
from jax import lax
import numpy as np

D_MODEL = 1024
BATCH = 4
SEQ = 4096
DEPTH = 2
DEC_BATCH = 32
DEC_SEQ = 8
PAST_LEN = 16384
PAGE_SIZE = 128

MIX_WIDTH = D_MODEL
W_A = MIX_WIDTH // 4
W_B = MIX_WIDTH // 4
W_C = MIX_WIDTH // 4
W_D = MIX_WIDTH - W_A - W_B - W_C
CONV_A = 3
POOL_WINDOWS = (2, 4, 8, 16)
POOL_GROUPS = 4
POOL_GW = W_B // POOL_GROUPS
POOL_BUF = 15
HEAD_DIM_C = 64
N_HEADS_C = W_C // HEAD_DIM_C
ROT_FRACTION = 4
ROPE_THETA = 500000.0
IDX_HEADS = 4
IDX_DIM = 64
TOPK_MAX = 256
Q_BLOCK = 128
SSM_HEAD_DIM = 64
SSM_HEADS = W_D // SSM_HEAD_DIM
SSM_GROUPS = 2
SSM_STATE = 128
SSM_CONV = 4
SSM_CHUNK = 128
XBC_DIM = W_D + 2 * SSM_GROUPS * SSM_STATE
PEER_HEADS = 8
PEER_NKEYS = 128
N_EXPERTS = PEER_NKEYS * PEER_NKEYS
PEER_QDIM = 256
PEER_HALF = PEER_QDIM // 2
PEER_TOPK = 16
PEER_BLOCK = 256
PROJ_SIZES = (W_A, W_A, W_A, W_B, W_C, W_C, W_C, IDX_HEADS * IDX_DIM, IDX_DIM, IDX_HEADS, W_D, XBC_DIM, SSM_HEADS)
P_TOTAL = 3 * W_A + W_B + 3 * W_C + IDX_HEADS * IDX_DIM + IDX_DIM + IDX_HEADS + W_D + XBC_DIM + SSM_HEADS
RMS_EPS = 1e-6
F32 = jnp.float32

kernel_name = 'hymba_style_conv_pool_dsa_ssd_peer_step'


def rmsnorm(x, g):
    xf = x.astype(F32)
    y = xf * lax.rsqrt(jnp.mean(xf * xf, axis=-1, keepdims=True) + RMS_EPS)
    return (y * g.astype(F32)).astype(x.dtype)


def causal_dwconv(u, buf, w):
    width = w.shape[0]
    L = u.shape[1]
    up = jnp.concatenate([buf.astype(u.dtype), u], axis=1)
    y = sum(up[:, j:j + L] * w[j] for j in range(width))
    return y, up[:, -(width - 1):]


def partial_rotary(x, pos):
    rot = x.shape[-1] // ROT_FRACTION
    half = rot // 2
    inv = ROPE_THETA ** (-jnp.arange(half, dtype=F32) * 2.0 / rot)
    ang = pos.astype(F32)[:, None] * inv[None, :]
    cos = jnp.cos(ang)[:, None, :]
    sin = jnp.sin(ang)[:, None, :]
    xf = x.astype(F32)
    x1 = xf[..., :half]
    x2 = xf[..., half:rot]
    out = jnp.concatenate([x1 * cos - x2 * sin, x1 * sin + x2 * cos, xf[..., rot:]], axis=-1)
    return out.astype(x.dtype)


def short_conv_mixer(h, gate_b, gate_c, buf, w):
    y, new_buf = causal_dwconv(gate_c * h, buf, w)
    return gate_b * y, new_buf


def pool_mixer(p, buf, w_grp, scale, pos):
    b, L, c = p.shape
    up = jnp.concatenate([buf.astype(p.dtype), p], axis=1)
    upf = up.astype(F32)
    cs = jnp.concatenate([jnp.zeros((b, 1, c), F32), jnp.cumsum(upf, axis=1)], axis=1)
    end = cs[:, POOL_BUF + 1:POOL_BUF + 1 + L]
    pieces = []
    for g, win in enumerate(POOL_WINDOWS):
        sl = slice(g * POOL_GW, (g + 1) * POOL_GW)
        start = cs[:, POOL_BUF + 1 - win:POOL_BUF + 1 - win + L, sl]
        cnt = jnp.minimum(win, pos + 1).astype(F32)[None, :, None]
        pieces.append((end[..., sl] - start) / cnt)
    pooled = jnp.concatenate(pieces, axis=-1)
    diff = (pooled - p.astype(F32)).reshape(b, L, POOL_GROUPS, POOL_GW)
    y = jnp.einsum('blgi,gio->blgo', diff, w_grp.astype(F32)).reshape(b, L, c) * scale.astype(F32)
    return y.astype(p.dtype), up[:, -POOL_BUF:]


def indexer_topk(qi, wi, ki, q_pos, key_pos, n_sel):
    dots = jnp.einsum('bthd,bsd->bths', qi.astype(F32), ki.astype(F32)) * (IDX_DIM ** -0.5)
    score = jnp.einsum('bth,bths->bts', wi.astype(F32) * (IDX_HEADS ** -0.5), jax.nn.relu(dots))
    causal = key_pos[None, :] <= q_pos[:, None]
    score = jnp.where(causal[None], score, -jnp.inf)
    _, top_i = lax.top_k(score, n_sel)
    valid = key_pos[top_i] <= q_pos[None, :, None]
    return top_i, valid


def attend_selected(q, k_sel, v_sel, valid):
    s = jnp.einsum('bthd,btkhd->bthk', q.astype(F32), k_sel.astype(F32)) * (q.shape[-1] ** -0.5)
    s = jnp.where(valid[:, :, None, :], s, -jnp.inf)
    pr = jax.nn.softmax(s, axis=-1)
    o = jnp.einsum('bthk,btkhd->bthd', pr, v_sel.astype(F32))
    return o.astype(q.dtype)


def dsa_prompt(q, k, v, qi, ki, wi, pos):
    b, L = q.shape[:2]
    n_sel = min(TOPK_MAX, L // 4)
    blk = min(Q_BLOCK, L)
    n_blk = L // blk

    def one_block(j):
        t0 = j * blk
        take = lambda a: lax.dynamic_slice_in_dim(a, t0, blk, axis=1)
        q_pos = lax.dynamic_slice_in_dim(pos, t0, blk, axis=0)
        top_i, valid = indexer_topk(take(qi), take(wi), ki, q_pos, pos, n_sel)
        k_sel = jax.vmap(lambda rows, idx: rows[idx])(k, top_i)
        v_sel = jax.vmap(lambda rows, idx: rows[idx])(v, top_i)
        return attend_selected(take(q), k_sel, v_sel, valid)

    out = lax.map(one_block, jnp.arange(n_blk))
    return jnp.swapaxes(out, 0, 1).reshape(b, L, N_HEADS_C, HEAD_DIM_C)


def gather_paged(cache_l, page_table, new_rows, idx):
    db = idx.shape[0]
    past_len = page_table.shape[1] * PAGE_SIZE
    pidx = jnp.minimum(idx, past_len - 1)
    phys = jnp.take_along_axis(page_table, (pidx // PAGE_SIZE).reshape(db, -1), axis=1).reshape(idx.shape)
    past = cache_l[phys, pidx % PAGE_SIZE].astype(new_rows.dtype)
    nidx = jnp.clip(idx - past_len, 0, new_rows.shape[1] - 1)
    new = jax.vmap(lambda rows, i: rows[i])(new_rows, nidx)
    in_past = (idx < past_len).reshape(idx.shape + (1,) * (new.ndim - idx.ndim))
    return jnp.where(in_past, past, new)


def dsa_sample(q, k, v, qi, ki, wi, pos, cache_k_l, cache_v_l, cache_kidx_l, page_table):
    db, T = q.shape[:2]
    past_len = page_table.shape[1] * PAGE_SIZE
    n_keys = past_len + T
    n_sel = min(TOPK_MAX, n_keys // 4)
    ki_past = cache_kidx_l[page_table].reshape(db, past_len, IDX_DIM).astype(ki.dtype)
    ki_all = jnp.concatenate([ki_past, ki], axis=1)
    top_i, valid = indexer_topk(qi, wi, ki_all, pos, jnp.arange(n_keys, dtype=jnp.int32), n_sel)
    k_sel = gather_paged(cache_k_l, page_table, k, top_i)
    v_sel = gather_paged(cache_v_l, page_table, v, top_i)
    return attend_selected(q, k_sel, v_sel, valid)


def ssd_scan(x, dt, A, Bm, Cm, h0):
    b, L, H, P = x.shape
    N = Bm.shape[-1]
    Q = min(SSM_CHUNK, L)
    pad = (-L) % Q
    nc = (L + pad) // Q
    rep = H // Bm.shape[2]
    padf = lambda a: jnp.pad(a, [(0, 0), (0, pad)] + [(0, 0)] * (a.ndim - 2))
    Bc = padf(jnp.repeat(Bm.astype(F32), rep, axis=2)).reshape(b, nc, Q, H, N)
    Cc = padf(jnp.repeat(Cm.astype(F32), rep, axis=2)).reshape(b, nc, Q, H, N)
    xdt = padf(x.astype(F32) * dt[..., None]).reshape(b, nc, Q, H, P)
    dA = padf(dt * A).reshape(b, nc, Q, H)
    cs = jnp.cumsum(dA, axis=2)
    seg = cs[:, :, :, None, :] - cs[:, :, None, :, :]
    causal = jnp.tril(jnp.ones((Q, Q), bool))[None, None, :, :, None]
    Lmat = jnp.exp(jnp.where(causal, seg, -jnp.inf))
    y_diag = jnp.einsum('bcqhn,bcshn,bcqsh,bcshp->bcqhp', Cc, Bc, Lmat, xdt)
    decay_to_end = jnp.exp(cs[:, :, -1:, :] - cs)
    chunk_states = jnp.einsum('bcshn,bcsh,bcshp->bchpn', Bc, decay_to_end, xdt)
    chunk_decay = jnp.exp(cs[:, :, -1, :])

    def step(h, inp):
        s_c, d_c = inp
        return h * d_c[:, :, None, None] + s_c, h

    h_final, h_prev = lax.scan(step, h0.astype(F32),
                               (jnp.moveaxis(chunk_states, 1, 0), jnp.moveaxis(chunk_decay, 1, 0)))
    h_prev = jnp.moveaxis(h_prev, 0, 1)
    y_off = jnp.einsum('bcqhn,bchpn,bcqh->bcqhp', Cc, h_prev, jnp.exp(cs))
    y = (y_diag + y_off).reshape(b, nc * Q, H, P)[:, :L]
    return y.astype(x.dtype), h_final.astype(h0.dtype)


def mamba2_mixer(z, xbc, dt_raw, conv_buf, ssm_h, conv_w, conv_b, dt_bias, a_log, d_skip, norm_g):
    b, L, _ = xbc.shape
    xc, new_conv = causal_dwconv(xbc, conv_buf, conv_w)
    xc = jax.nn.silu(xc + conv_b)
    gn = SSM_GROUPS * SSM_STATE
    xs = xc[..., :W_D].reshape(b, L, SSM_HEADS, SSM_HEAD_DIM)
    Bm = xc[..., W_D:W_D + gn].reshape(b, L, SSM_GROUPS, SSM_STATE)
    Cm = xc[..., W_D + gn:].reshape(b, L, SSM_GROUPS, SSM_STATE)
    dt = jax.nn.softplus(dt_raw.astype(F32) + dt_bias.astype(F32))
    A = -jnp.exp(a_log.astype(F32))
    y, new_h = ssd_scan(xs, dt, A, Bm, Cm, ssm_h)
    y = (y + xs * d_skip[:, None].astype(xs.dtype)).reshape(b, L, W_D)
    y = rmsnorm(y * jax.nn.silu(z), norm_g)
    return y, new_conv, new_h


def peer_ffn(x, wq, k1, k2, u, v):
    shp = x.shape
    xt = x.reshape(-1, shp[-1])
    n = xt.shape[0]
    blk = min(PEER_BLOCK, n)
    pad = (-n) % blk
    xb_all = jnp.pad(xt, ((0, pad), (0, 0))).reshape(-1, blk, shp[-1])

    def one_block(xb):
        q = (xb @ wq).astype(F32).reshape(blk, PEER_HEADS, 2, PEER_HALF)
        s1 = jnp.einsum('thd,kd->thk', q[:, :, 0], k1.astype(F32))
        s2 = jnp.einsum('thd,kd->thk', q[:, :, 1], k2.astype(F32))
        t1, i1 = lax.top_k(s1, PEER_TOPK)
        t2, i2 = lax.top_k(s2, PEER_TOPK)
        cand = (t1[..., :, None] + t2[..., None, :]).reshape(blk, PEER_HEADS, PEER_TOPK * PEER_TOPK)
        cidx = (i1[..., :, None] * PEER_NKEYS + i2[..., None, :]).reshape(blk, PEER_HEADS, PEER_TOPK * PEER_TOPK)
        ts, tpos = lax.top_k(cand, PEER_TOPK)
        eidx = jnp.take_along_axis(cidx, tpos, axis=-1)
        g = jax.nn.softmax(ts, axis=-1)
        ue = u[eidx].astype(F32)
        ve = v[eidx].astype(F32)
        act = jax.nn.gelu(jnp.einsum('thkd,td->thk', ue, xb.astype(F32)))
        return jnp.einsum('thk,thkd->td', g * act, ve).astype(x.dtype)

    y = lax.map(one_block, xb_all).reshape(-1, shp[-1])[:n]
    return y.reshape(shp)


def trunk_layer(x, pos, attend, conv_a_buf, pool_buf, conv_d_buf, ssm_h,
                norm_mix, w_in, conv_a_w, pool_w, pool_scale, ssm_conv_w, ssm_conv_b,
                ssm_dt_bias, ssm_a_log, ssm_d, ssm_norm, w_out, norm_ffn,
                peer_wq, peer_k1, peer_k2, peer_u, peer_v):
    b, L, _ = x.shape
    xn = rmsnorm(x, norm_mix)
    proj = jnp.einsum('bld,dp->blp', xn, w_in)
    (a_h, a_b, a_c, b_p, c_q, c_k, c_v, c_qi, c_ki, c_wi, d_z, d_xbc, d_dt) = jnp.split(
        proj, np.cumsum(PROJ_SIZES)[:-1].tolist(), axis=-1)
    y_a, new_conv_a = short_conv_mixer(a_h, a_b, a_c, conv_a_buf, conv_a_w)
    y_b, new_pool = pool_mixer(b_p, pool_buf, pool_w, pool_scale, pos)
    q = partial_rotary(c_q.reshape(b, L, N_HEADS_C, HEAD_DIM_C), pos)
    k = partial_rotary(c_k.reshape(b, L, N_HEADS_C, HEAD_DIM_C), pos)
    v = c_v.reshape(b, L, N_HEADS_C, HEAD_DIM_C)
    qi = partial_rotary(c_qi.reshape(b, L, IDX_HEADS, IDX_DIM), pos)
    ki = partial_rotary(c_ki[:, :, None, :], pos)[:, :, 0, :]
    y_c = attend(q, k, v, qi, ki, c_wi, pos).reshape(b, L, W_C)
    y_d, new_conv_d, new_h = mamba2_mixer(d_z, d_xbc, d_dt, conv_d_buf, ssm_h, ssm_conv_w, ssm_conv_b,
                                          ssm_dt_bias, ssm_a_log, ssm_d, ssm_norm)
    mix = jnp.concatenate([y_a, y_b, y_c, y_d], axis=-1)
    h = x + jnp.einsum('blm,md->bld', mix, w_out)
    out = h + peer_ffn(rmsnorm(h, norm_ffn), peer_wq, peer_k1, peer_k2, peer_u, peer_v)
    return out, (new_conv_a, new_pool, k, v, ki, new_conv_d, new_h)


def setup_inputs(seed: int = 0) -> dict:
    key = jax.random.key(seed)
    ks = jax.random.split(key, 40)
    nrm = lambda i, shape, scale: jax.random.normal(ks[i], shape, F32) * scale
    n_pages = PAST_LEN // PAGE_SIZE
    n_used = DEC_BATCH * n_pages
    n_pool = (n_used * 5) // 4
    page_table = jax.random.permutation(ks[0], n_pool)[:n_used].reshape(DEC_BATCH, n_pages).astype(jnp.int32)
    dt0 = jnp.exp(jax.random.uniform(ks[1], (DEPTH, SSM_HEADS), F32, math.log(1e-3), math.log(1e-1)))
    ssm_dt_bias = dt0 + jnp.log(-jnp.expm1(-dt0))
    ssm_a_log = jnp.log(jax.random.uniform(ks[2], (DEPTH, SSM_HEADS), F32, 1.0, 16.0))
    return {
        'x_prompt': nrm(3, (BATCH, SEQ, D_MODEL), 1.0),
        'x_sample': nrm(4, (DEC_BATCH, DEC_SEQ, D_MODEL), 1.0),
        'state_conv_a': nrm(5, (DEPTH, DEC_BATCH, CONV_A - 1, W_A), 1.0),
        'state_pool': nrm(6, (DEPTH, DEC_BATCH, POOL_BUF, W_B), 1.0),
        'cache_k': nrm(7, (DEPTH, n_pool, PAGE_SIZE, N_HEADS_C, HEAD_DIM_C), 1.0),
        'cache_v': nrm(8, (DEPTH, n_pool, PAGE_SIZE, N_HEADS_C, HEAD_DIM_C), 1.0),
        'cache_kidx': nrm(9, (DEPTH, n_pool, PAGE_SIZE, IDX_DIM), 1.0),
        'page_table': page_table,
        'state_conv_d': nrm(10, (DEPTH, DEC_BATCH, SSM_CONV - 1, XBC_DIM), 1.0),
        'state_ssm': nrm(11, (DEPTH, DEC_BATCH, SSM_HEADS, SSM_HEAD_DIM, SSM_STATE), 0.1),
        'norm_mix': 1.0 + nrm(12, (DEPTH, D_MODEL), 0.01),
        'w_in': nrm(13, (DEPTH, D_MODEL, P_TOTAL), D_MODEL ** -0.5),
        'conv_a_w': nrm(14, (DEPTH, CONV_A, W_A), CONV_A ** -0.5),
        'pool_w': nrm(15, (DEPTH, POOL_GROUPS, POOL_GW, POOL_GW), POOL_GW ** -0.5),
        'pool_scale': 1.0 + nrm(16, (DEPTH, W_B), 0.1),
        'ssm_conv_w': nrm(17, (DEPTH, SSM_CONV, XBC_DIM), SSM_CONV ** -0.5),
        'ssm_conv_b': nrm(18, (DEPTH, XBC_DIM), 0.01),
        'ssm_dt_bias': ssm_dt_bias,
        'ssm_a_log': ssm_a_log,
        'ssm_d': 1.0 + nrm(19, (DEPTH, SSM_HEADS), 0.1),
        'ssm_norm': 1.0 + nrm(20, (DEPTH, W_D), 0.01),
        'w_out': nrm(21, (DEPTH, MIX_WIDTH, D_MODEL), MIX_WIDTH ** -0.5),
        'norm_ffn': 1.0 + nrm(22, (DEPTH, D_MODEL), 0.01),
        'peer_wq': nrm(23, (DEPTH, D_MODEL, PEER_HEADS * PEER_QDIM), D_MODEL ** -0.5),
        'peer_k1': nrm(24, (DEPTH, PEER_NKEYS, PEER_HALF), PEER_HALF ** -0.5),
        'peer_k2': nrm(25, (DEPTH, PEER_NKEYS, PEER_HALF), PEER_HALF ** -0.5),
        'peer_u': nrm(26, (DEPTH, N_EXPERTS, D_MODEL), D_MODEL ** -0.5),
        'peer_v': nrm(27, (DEPTH, N_EXPERTS, D_MODEL), PEER_HEADS ** -0.5),
        'norm_final': 1.0 + nrm(28, (D_MODEL,), 0.01),
    }


def reference(x_prompt, x_sample, state_conv_a, state_pool, cache_k, cache_v, cache_kidx, page_table,
              state_conv_d, state_ssm, norm_mix, w_in, conv_a_w, pool_w, pool_scale, ssm_conv_w,
              ssm_conv_b, ssm_dt_bias, ssm_a_log, ssm_d, ssm_norm, w_out, norm_ffn, peer_wq,
              peer_k1, peer_k2, peer_u, peer_v, norm_final):
    bp, lp, _ = x_prompt.shape
    _, ls, _ = x_sample.shape
    past_len = page_table.shape[1] * PAGE_SIZE
    pos_p = jnp.arange(lp, dtype=jnp.int32)
    pos_s = past_len + jnp.arange(ls, dtype=jnp.int32)
    dtp = x_prompt.dtype
    hp, hs = x_prompt, x_sample
    new_p = [[] for _ in range(7)]
    new_s = [[] for _ in range(7)]
    for l in range(DEPTH):
        lw = (norm_mix[l], w_in[l], conv_a_w[l], pool_w[l], pool_scale[l], ssm_conv_w[l], ssm_conv_b[l],
              ssm_dt_bias[l], ssm_a_log[l], ssm_d[l], ssm_norm[l], w_out[l], norm_ffn[l],
              peer_wq[l], peer_k1[l], peer_k2[l], peer_u[l], peer_v[l])
        hp, st_p = trunk_layer(hp, pos_p, dsa_prompt,
                               jnp.zeros((bp, CONV_A - 1, W_A), dtp),
                               jnp.zeros((bp, POOL_BUF, W_B), dtp),
                               jnp.zeros((bp, SSM_CONV - 1, XBC_DIM), dtp),
                               jnp.zeros((bp, SSM_HEADS, SSM_HEAD_DIM, SSM_STATE), state_ssm.dtype),
                               *lw)
        attend_s = functools.partial(dsa_sample, cache_k_l=cache_k[l], cache_v_l=cache_v[l],
                                     cache_kidx_l=cache_kidx[l], page_table=page_table)
        hs, st_s = trunk_layer(hs, pos_s, attend_s, state_conv_a[l], state_pool[l], state_conv_d[l],
                               state_ssm[l], *lw)
        for i in range(7):
            new_p[i].append(st_p[i])
            new_s[i].append(st_s[i])
    y_prompt = rmsnorm(hp, norm_final)
    y_sample = rmsnorm(hs, norm_final)
    p_conv_a, p_pool, p_k, p_v, p_kidx, p_conv_d, p_ssm = [jnp.stack(a, axis=0) for a in new_p]
    s_conv_a, s_pool, s_k, s_v, s_kidx, s_conv_d, s_ssm = [jnp.stack(a, axis=0) for a in new_s]
    return (y_prompt, y_sample, p_conv_a, p_pool, p_k, p_v, p_kidx, p_conv_d, p_ssm,
            s_conv_a, s_pool, s_k, s_v, s_kidx, s_conv_d, s_ssm)
```

```python
import functools
import math

import jax
import jax.numpy as jnp
import numpy as np
from jax import lax
from jax.experimental import pallas as pl
from jax.experimental.pallas import tpu as pltpu

F32 = jnp.float32
BF16 = jnp.bfloat16
I32 = jnp.int32
HIGHEST = lax.Precision.HIGHEST

D_MODEL = 1024
PAGE_SIZE = 128
W_A = W_B = W_C = W_D = 256
CONV_A = 3
POOL_WINDOWS = (2, 4, 8, 16)
POOL_GROUPS = 4
POOL_GW = W_B // POOL_GROUPS
POOL_BUF = 15
HEAD_DIM_C = 64
N_HEADS_C = W_C // HEAD_DIM_C
ROT_FRACTION = 4
ROPE_THETA = 500000.0
IDX_HEADS = 4
IDX_DIM = 64
TOPK_MAX = 256
SSM_HEAD_DIM = 64
SSM_HEADS = W_D // SSM_HEAD_DIM
SSM_GROUPS = 2
SSM_STATE = 128
SSM_CONV = 4
XBC_DIM = W_D + 2 * SSM_GROUPS * SSM_STATE
PEER_HEADS = 8
PEER_NKEYS = 128
PEER_QDIM = 256
PEER_HALF = PEER_QDIM // 2
PEER_TOPK = 16
PROJ_SIZES = (W_A, W_A, W_A, W_B, W_C, W_C, W_C, IDX_HEADS * IDX_DIM, IDX_DIM, IDX_HEADS,
              W_D, XBC_DIM, SSM_HEADS)
RMS_EPS = 1e-6

LANE = 128
CHUNK = 128
HALO = 16
NEG_BIG = -1e30
VMEM_LIMIT = 48 * 1024 * 1024

SEG_A = (0, 768)
SEG_B = (768, 1024)
SEG_Q = (1024, 1280)
SEG_K = (1280, 1536)
SEG_V = (1536, 1792)
SEG_QI = (1792, 2048)
SEG_KI = (2048, 2176)
SEG_WI = (2176, 2304)
SEG_Z = (2304, 2560)
SEG_XBC = (2560, 3328)
SEG_DT = (3328, 3456)
P_PACKED = 3456


def _sigmoid(x):
    return 1.0 / (1.0 + jnp.exp(-x))


def _silu(x):
    return x * _sigmoid(x)


def _softplus(x):
    return jnp.maximum(x, 0.0) + jnp.log(1.0 + jnp.exp(-jnp.abs(x)))


def _gelu_tanh(x):
    c = math.sqrt(2.0 / math.pi)
    return 0.5 * x * (1.0 + jnp.tanh(c * (x + 0.044715 * (x * x * x))))


def _rms(x, g):
    return x * lax.rsqrt(jnp.mean(x * x, axis=-1, keepdims=True) + RMS_EPS) * g


def _sortable_key(s):
    b = pltpu.bitcast(s, I32)
    return b ^ ((b >> 31) & jnp.int32(0x7FFFFFFF))


_NEG_INF_KEY = int(np.array(-np.inf, np.float32).view(np.int32) ^ 0x7FFFFFFF)
_INT_MIN = -(2 ** 31)


def _rot(a, c, s_lo, s_hi):
    n = a.shape[-1]
    half = HEAD_DIM_C // ROT_FRACTION // 2
    return a * c + pltpu.roll(a, n - half, 1) * s_lo + pltpu.roll(a, half, 1) * s_hi


def _inproj_kernel(x_ref, g_ref, w_ref, cos_ref, slo_ref, shi_ref,
                   oa_ref, ob_ref, q_ref, k_ref, v_ref, qi_ref, ki_ref, wi_ref, z_ref, xbc_ref, dt_ref):
    xn = _rms(x_ref[...], g_ref[...]).astype(BF16)

    def proj(seg):
        return jnp.dot(xn, w_ref[:, seg[0]:seg[1]], preferred_element_type=F32)

    c, s_lo, s_hi = cos_ref[...], slo_ref[...], shi_ref[...]
    oa_ref[...] = proj(SEG_A)
    ob_ref[...] = proj(SEG_B)
    q_ref[...] = _rot(proj(SEG_Q), c, s_lo, s_hi)
    k_ref[...] = _rot(proj(SEG_K), c, s_lo, s_hi)
    v_ref[...] = proj(SEG_V)
    qi_ref[...] = _rot(proj(SEG_QI), c, s_lo, s_hi)
    ki = _rot(proj(SEG_KI), c[:, :LANE], s_lo[:, :LANE], s_hi[:, :LANE])
    ki_ref[...] = ki[:, :IDX_DIM]
    wi_ref[...] = proj(SEG_WI)
    z_ref[...] = proj(SEG_Z)
    xbc_ref[...] = proj(SEG_XBC)
    dt_ref[...] = proj(SEG_DT)


def _inproj(x, g, w_packed, tabs, tile):
    n = x.shape[0]
    cos_t, slo_t, shi_t = tabs
    nrep = cos_t.shape[0] // tile
    row = lambda w: pl.BlockSpec((tile, w), lambda i: (i, 0))
    tab = pl.BlockSpec((tile, W_C), lambda i: (i % nrep, 0))
    widths = (768, 256, 256, 256, 256, 256, IDX_DIM, LANE, 256, 768, LANE)
    return pl.pallas_call(
        _inproj_kernel,
        grid=(n // tile,),
        in_specs=[row(D_MODEL), pl.BlockSpec((1, D_MODEL), lambda i: (0, 0)),
                  pl.BlockSpec((D_MODEL, P_PACKED), lambda i: (0, 0)), tab, tab, tab],
        out_specs=[row(w) for w in widths],
        out_shape=[jax.ShapeDtypeStruct((n, w), F32) for w in widths],
        compiler_params=pltpu.CompilerParams(dimension_semantics=("arbitrary",),
                                             vmem_limit_bytes=VMEM_LIMIT),
        name="inproj",
    )(x, g, w_packed, cos_t, slo_t, shi_t)


def _rot_tables(pos):
    rot = HEAD_DIM_C // ROT_FRACTION
    half = rot // 2
    inv = ROPE_THETA ** (-jnp.arange(half, dtype=F32) * 2.0 / rot)
    ang = pos.astype(F32)[:, None] * inv[None, :]
    cos, sin = jnp.cos(ang), jnp.sin(ang)
    n = pos.shape[0]
    one = jnp.ones((n, HEAD_DIM_C - rot), F32)
    zero8 = jnp.zeros((n, half), F32)
    zero48 = jnp.zeros((n, HEAD_DIM_C - rot), F32)
    c = jnp.concatenate([cos, cos, one], axis=1)
    s_lo = jnp.concatenate([-sin, zero8, zero48], axis=1)
    s_hi = jnp.concatenate([zero8, sin, zero48], axis=1)
    rep = W_C // HEAD_DIM_C
    return tuple(jnp.tile(t, (1, rep)) for t in (c, s_lo, s_hi))


def _pack_w_in(w):
    offs = np.cumsum((0,) + PROJ_SIZES)
    seg = lambda i: w[:, offs[i]:offs[i + 1]]
    pad = lambda a: jnp.pad(a, ((0, 0), (0, LANE - a.shape[1])))
    cols = [seg(0), seg(1), seg(2), seg(3), seg(4), seg(5), seg(6), seg(7),
            pad(seg(8)), pad(seg(9)), seg(10), seg(11), pad(seg(12))]
    return jnp.concatenate(cols, axis=1).astype(BF16)


def _mixer_kernel(oa_ref, ob_ref, z_ref, xbc_ref, dt_ref, sa_ref, sb_ref, sd_ref, h0_ref,
                  caw_ref, pw_ref, pscale_ref, cdw_ref, cdb_ref, dtb_ref, alog_ref, dskip_ref, ng_ref,
                  expand_ref,
                  y_ref, ua_ref, hout_ref,
                  halo_a, halo_b, halo_d, h_scr, *, qin, pos0):
    c = pl.program_id(1)

    @pl.when(c == 0)
    def _():
        halo_a[...] = sa_ref[...]
        halo_b[...] = sb_ref[...]
        halo_d[...] = sd_ref[...]
        h_scr[...] = h0_ref[...]

    def chunk(ref):
        blk = ref[...]
        if qin == CHUNK:
            return blk
        return jnp.concatenate([blk, jnp.zeros((CHUNK - qin, blk.shape[1]), F32)], axis=0)

    def with_halo(halo_ref, cur):
        return jnp.concatenate([halo_ref[...], cur], axis=0)

    def shifted(ext, s):
        r = ext if s == 0 else pltpu.roll(ext, s, 0)
        return r[HALO:]

    rows = lax.broadcasted_iota(I32, (CHUNK, 1), 0)
    row_valid = rows < qin

    oa = chunk(oa_ref)
    a_h, a_b, a_c = oa[:, :W_A], oa[:, W_A:2 * W_A], oa[:, 2 * W_A:]
    u = a_c * a_h
    ext_a = with_halo(halo_a, u)
    conv_a = sum(caw_ref[j:j + 1, :] * shifted(ext_a, CONV_A - 1 - j) for j in range(CONV_A))
    y_a = a_b * conv_a
    halo_a[...] = u[CHUNK - HALO:]
    ua_ref[...] = u[qin - 8:qin]

    p = chunk(ob_ref)
    ext_b = with_halo(halo_b, p)
    w2 = ext_b + pltpu.roll(ext_b, 1, 0)
    w4 = w2 + pltpu.roll(w2, 2, 0)
    w8 = w4 + pltpu.roll(w4, 4, 0)
    w16 = w8 + pltpu.roll(w8, 8, 0)
    lane = lax.broadcasted_iota(I32, (1, W_B), 1)
    grp = lane // POOL_GW
    wsum = jnp.where(grp == 0, w2[HALO:], jnp.where(grp == 1, w4[HALO:], jnp.where(grp == 2, w8[HALO:], w16[HALO:])))
    win = jnp.where(grp == 0, 2, jnp.where(grp == 1, 4, jnp.where(grp == 2, 8, 16)))
    pos = pos0 + c * CHUNK + rows
    cnt = jnp.minimum(win, pos + 1).astype(F32)
    diff = wsum / cnt - p
    y_b = jnp.dot(diff, pw_ref[...], preferred_element_type=F32) * pscale_ref[...]
    halo_b[...] = p[CHUNK - HALO:]

    xbc = chunk(xbc_ref)
    ext_d = with_halo(halo_d, xbc)
    xc = sum(cdw_ref[j:j + 1, :] * shifted(ext_d, SSM_CONV - 1 - j) for j in range(SSM_CONV))
    xc = _silu(xc + cdb_ref[...])
    halo_d[...] = xbc[CHUNK - HALO:]
    gn = SSM_GROUPS * SSM_STATE
    xs, bm, cm = xc[:, :W_D], xc[:, W_D:W_D + gn], xc[:, W_D + gn:]

    dt = jnp.where(row_valid, _softplus(chunk(dt_ref) + dtb_ref[...]), 0.0)
    d_a = dt * (-jnp.exp(alog_ref[...]))
    r_i = lax.broadcasted_iota(I32, (CHUNK, CHUNK), 0)
    c_i = lax.broadcasted_iota(I32, (CHUNK, CHUNK), 1)
    causal = r_i >= c_i
    cs = jnp.dot(causal.astype(F32), d_a, preferred_element_type=F32, precision=HIGHEST)
    cs_t = cs.T
    cs_last = cs[CHUNK - 1:CHUNK, :]
    expand = expand_ref[...]
    to_lanes = lambda a: jnp.dot(a, expand, preferred_element_type=F32, precision=HIGHEST)
    xdt = xs * to_lanes(dt)
    xdt_dec_t = (xdt * to_lanes(jnp.exp(cs_last - cs))).T
    ecs = to_lanes(jnp.exp(cs))
    h_dec = jnp.exp(cs_last)
    lane_d = lax.broadcasted_iota(I32, (1, 2 * SSM_HEAD_DIM), 1)
    nt = (((1,), (1,)), ((), ()))
    y_diag, y_offd = [], []
    for g in range(SSM_GROUPS):
        sl = slice(g * SSM_STATE, (g + 1) * SSM_STATE)
        b_g, c_g = bm[:, sl], cm[:, sl]
        gmat = lax.dot_general(c_g, b_g, nt, preferred_element_type=F32)
        xdt_g = xdt[:, g * 2 * SSM_HEAD_DIM:(g + 1) * 2 * SSM_HEAD_DIM]
        h_prev = h_scr[g * 2 * SSM_HEAD_DIM:(g + 1) * 2 * SSM_HEAD_DIM, :]
        y_off = lax.dot_general(c_g, h_prev, nt, preferred_element_type=F32)
        y_pair = []
        for j in range(2):
            h = 2 * g + j
            seg = cs[:, h:h + 1] - cs_t[h:h + 1, :]
            lmat = jnp.exp(jnp.where(causal, seg, -jnp.inf))
            y_pair.append(jnp.dot(gmat * lmat, xdt_g, preferred_element_type=F32))
            rs = slice(h * SSM_HEAD_DIM, (h + 1) * SSM_HEAD_DIM)
            upd = jnp.dot(xdt_dec_t[rs, :], b_g, preferred_element_type=F32)
            h_scr[rs, :] = h_scr[rs, :] * h_dec[:, h:h + 1] + upd
        y_diag.append(jnp.where(lane_d < SSM_HEAD_DIM, y_pair[0], y_pair[1]))
        y_offd.append(y_off)
    hout_ref[...] = h_scr[...]
    y_d = jnp.concatenate(y_diag, axis=1) + jnp.concatenate(y_offd, axis=1) * ecs + xs * dskip_ref[...]
    y_d = _rms(y_d * _silu(chunk(z_ref)), ng_ref[...])

    y = jnp.concatenate([y_a, y_b, y_d], axis=1)
    y_ref[...] = y[:qin]


def _mixer(oa, ob, z, xbc, dt, st_a, st_b, st_d, h0, prm, *, qin, pos0):
    b, l, _ = oa.shape
    nc = l // qin
    seq = lambda w: pl.BlockSpec((None, qin, w), lambda i, c: (i, c, 0))
    per_b = lambda r, w: pl.BlockSpec((None, r, w), lambda i, c: (i, 0, 0))
    full = lambda a: pl.BlockSpec(a.shape, lambda i, c: (0, 0))
    return pl.pallas_call(
        functools.partial(_mixer_kernel, qin=qin, pos0=pos0),
        grid=(b, nc),
        in_specs=[seq(3 * W_A), seq(W_B), seq(W_D), seq(XBC_DIM), seq(LANE),
                  per_b(HALO, W_A), per_b(HALO, W_B), per_b(HALO, XBC_DIM), per_b(W_D, SSM_STATE)]
                 + [full(a) for a in prm],
        out_specs=[seq(W_A + W_B + W_D), per_b(8, W_A), per_b(W_D, SSM_STATE)],
        out_shape=[jax.ShapeDtypeStruct((b, l, W_A + W_B + W_D), F32),
                   jax.ShapeDtypeStruct((b, 8, W_A), F32),
                   jax.ShapeDtypeStruct((b, W_D, SSM_STATE), F32)],
        scratch_shapes=[pltpu.VMEM((HALO, W_A), F32), pltpu.VMEM((HALO, W_B), F32),
                        pltpu.VMEM((HALO, XBC_DIM), F32), pltpu.VMEM((W_D, SSM_STATE), F32)],
        compiler_params=pltpu.CompilerParams(dimension_semantics=("arbitrary", "arbitrary"),
                                             vmem_limit_bytes=VMEM_LIMIT),
        name="mixer",
    )(oa, ob, z, xbc, dt, st_a, st_b, st_d, h0, *prm)


_NT = (((1,), (1,)), ((), ()))
_IDX_SCALE = (IDX_HEADS ** -0.5) * (IDX_DIM ** -0.5)
_ATT_SCALE = HEAD_DIM_C ** -0.5


def _index_scores(qi, wi_scaled, ki):
    sc = None
    for h in range(IDX_HEADS):
        d = lax.dot_general(qi[:, h * IDX_DIM:(h + 1) * IDX_DIM], ki, _NT, preferred_element_type=F32)
        term = wi_scaled[:, h:h + 1] * jnp.maximum(d, 0.0)
        sc = term if sc is None else sc + term
    return sc


def _bit_search(count_ge, shape, n_sel):
    def body(i, tau_u):
        cand_u = tau_u | (jnp.int32(1) << (31 - i))
        cnt = count_ge(cand_u ^ jnp.int32(_INT_MIN))
        return jnp.where(cnt >= n_sel, cand_u, tau_u)
    tau_u = lax.fori_loop(0, 32, body, jnp.zeros(shape, I32))
    return tau_u ^ jnp.int32(_INT_MIN)


def _select_chunk(key, tau, budget, run, tri):
    gt = key > tau
    eq = key == tau
    pre = jnp.dot(jnp.where(eq, 1.0, 0.0).astype(BF16), tri, preferred_element_type=F32)
    sel = (gt | (eq & (run + pre <= budget))) & (key != _NEG_INF_KEY)
    return sel, run + pre[:, CHUNK - 1:CHUNK]


def _attend_chunk(q, k, v, sel, m, l, acc):
    m2, l2, acc2 = [], [], []
    for h in range(N_HEADS_C):
        hs = slice(h * HEAD_DIM_C, (h + 1) * HEAD_DIM_C)
        s = lax.dot_general(q[:, hs], k[:, hs], _NT, preferred_element_type=F32) * _ATT_SCALE
        s = jnp.where(sel, s, NEG_BIG)
        m_new = jnp.maximum(m[h], jnp.max(s, axis=1, keepdims=True))
        alpha = jnp.exp(m[h] - m_new)
        p = jnp.where(sel, jnp.exp(s - m_new), 0.0)
        l2.append(alpha * l[h] + jnp.sum(p, axis=1, keepdims=True))
        acc2.append(alpha * acc[h] + jnp.dot(p, v[:, hs], preferred_element_type=F32))
        m2.append(m_new)
    return m2, l2, acc2


def _prefix_tri():
    r_i = lax.broadcasted_iota(I32, (CHUNK, CHUNK), 0)
    c_i = lax.broadcasted_iota(I32, (CHUNK, CHUNK), 1)
    return jnp.where(r_i <= c_i, 1.0, 0.0).astype(BF16)


def _dsa_prompt_kernel(qi_ref, wi_ref, q_ref, ki_ref, k_ref, v_ref, o_ref, key_scr, *, n_sel):
    j = pl.program_id(1)
    nk = j + 1
    qi = qi_ref[...]
    wi = wi_ref[...] * _IDX_SCALE
    q_pos = j * CHUNK + lax.broadcasted_iota(I32, (CHUNK, 1), 0)
    k_off = lax.broadcasted_iota(I32, (1, CHUNK), 1)

    def score_body(kc, carry):
        start = pl.multiple_of(kc * CHUNK, CHUNK)
        sc = _index_scores(qi, wi, ki_ref[pl.ds(start, CHUNK), :])
        sc = jnp.where(kc * CHUNK + k_off <= q_pos, sc, -jnp.inf)
        key_scr[kc] = _sortable_key(sc)
        return carry

    lax.fori_loop(0, nk, score_body, 0)

    def count(pred):
        def body(kc, acc):
            return acc + jnp.where(pred(key_scr[kc]), 1, 0)
        acc = lax.fori_loop(0, nk, body, jnp.zeros((CHUNK, CHUNK), I32))
        return jnp.sum(acc, axis=1, keepdims=True)

    tau = _bit_search(lambda cand: count(lambda key: key >= cand), (CHUNK, 1), n_sel)
    budget = (n_sel - count(lambda key: key > tau)).astype(F32)

    q = q_ref[...]
    tri = _prefix_tri()

    def att_body(kc, carry):
        run, m, l, acc = carry
        start = pl.multiple_of(kc * CHUNK, CHUNK)
        sel, run = _select_chunk(key_scr[kc], tau, budget, run, tri)
        m, l, acc = _attend_chunk(q, k_ref[pl.ds(start, CHUNK), :], v_ref[pl.ds(start, CHUNK), :],
                                  sel, m, l, acc)
        return run, m, l, acc

    init = (jnp.zeros((CHUNK, 1), F32),
            [jnp.full((CHUNK, 1), NEG_BIG, F32)] * N_HEADS_C,
            [jnp.zeros((CHUNK, 1), F32)] * N_HEADS_C,
            [jnp.zeros((CHUNK, HEAD_DIM_C), F32)] * N_HEADS_C)
    _, _, l, acc = lax.fori_loop(0, nk, att_body, init)
    o_ref[...] = jnp.concatenate([acc[h] / l[h] for h in range(N_HEADS_C)], axis=1)


def _dsa_prompt(qi, wi, q, ki, k, v):
    b, l, _ = q.shape
    n_sel = min(TOPK_MAX, l // 4)
    blk = lambda w: pl.BlockSpec((None, CHUNK, w), lambda i, j: (i, j, 0))
    seq = lambda w: pl.BlockSpec((None, l, w), lambda i, j: (i, 0, 0))
    return pl.pallas_call(
        functools.partial(_dsa_prompt_kernel, n_sel=n_sel),
        grid=(b, l // CHUNK),
        in_specs=[blk(W_C), blk(LANE), blk(W_C), seq(IDX_DIM), seq(W_C), seq(W_C)],
        out_specs=blk(W_C),
        out_shape=jax.ShapeDtypeStruct((b, l, W_C), F32),
        scratch_shapes=[pltpu.VMEM((l // CHUNK, CHUNK, CHUNK), I32)],
        compiler_params=pltpu.CompilerParams(dimension_semantics=("arbitrary", "arbitrary"),
                                             vmem_limit_bytes=VMEM_LIMIT),
        name="dsa_prompt",
    )(qi, wi, q, ki, k, v)


def _dsa_sample_score_kernel(pt_ref, qi_ref, wi_ref, page_ref, kin_ref, key_ref, *, n_pages):
    del pt_ref
    p = pl.program_id(1)
    is_new = p == n_pages
    ki = jnp.where(is_new, kin_ref[...], page_ref[...])
    sc = _index_scores(qi_ref[...], wi_ref[...] * _IDX_SCALE, ki)
    t = lax.broadcasted_iota(I32, (sc.shape[0], 1), 0)
    k_off = lax.broadcasted_iota(I32, (1, CHUNK), 1)
    sc = jnp.where(jnp.logical_or(jnp.logical_not(is_new), k_off <= t), sc, -jnp.inf)
    key_ref[...] = _sortable_key(sc)


def _dsa_sample_scores(page_table, qi, wi, cache_kidx, layer, ki_new):
    b, t, _ = qi.shape
    n_pages = page_table.shape[1]
    last = n_pages - 1
    grid_spec = pltpu.PrefetchScalarGridSpec(
        num_scalar_prefetch=1,
        grid=(b, n_pages + 1),
        in_specs=[pl.BlockSpec((None, t, W_C), lambda i, p, pt: (i, 0, 0)),
                  pl.BlockSpec((None, t, LANE), lambda i, p, pt: (i, 0, 0)),
                  pl.BlockSpec((None, None, PAGE_SIZE, IDX_DIM),
                               lambda i, p, pt: (layer, pt[i, jnp.minimum(p, last)], 0, 0)),
                  pl.BlockSpec((None, CHUNK, IDX_DIM), lambda i, p, pt: (i, 0, 0))],
        out_specs=pl.BlockSpec((None, None, t, CHUNK), lambda i, p, pt: (i, p, 0, 0)),
    )
    return pl.pallas_call(
        functools.partial(_dsa_sample_score_kernel, n_pages=n_pages),
        grid_spec=grid_spec,
        out_shape=jax.ShapeDtypeStruct((b, n_pages + 1, t, CHUNK), I32),
        compiler_params=pltpu.CompilerParams(dimension_semantics=("arbitrary", "arbitrary"),
                                             vmem_limit_bytes=VMEM_LIMIT),
        name="dsa_sample_scores",
    )(page_table, qi, wi, cache_kidx, ki_new)


def _dsa_sample_attend_kernel(pt_ref, keys_ref, q_ref, kpage_ref, vpage_ref, knew_ref, vnew_ref, o_ref,
                              tau_scr, bud_scr, run_scr, m_scr, l_scr, acc_scr, *, n_pages, n_sel):
    del pt_ref
    p = pl.program_id(1)
    t = q_ref.shape[0]

    @pl.when(p == 0)
    def _():
        keys = keys_ref[...]

        def count(mask):
            per = jnp.sum(jnp.where(mask, 1, 0), axis=0)
            return jnp.sum(per, axis=1, keepdims=True)

        tau = _bit_search(lambda cand: count(keys >= cand[None]), (t, 1), n_sel)
        budget = (n_sel - count(keys > tau[None])).astype(F32)
        tau_scr[...] = jnp.broadcast_to(tau, tau_scr.shape)
        bud_scr[...] = jnp.broadcast_to(budget, bud_scr.shape)
        run_scr[...] = jnp.zeros_like(run_scr)
        m_scr[...] = jnp.full_like(m_scr, NEG_BIG)
        l_scr[...] = jnp.zeros_like(l_scr)
        acc_scr[...] = jnp.zeros_like(acc_scr)

    is_new = p == n_pages
    k = jnp.where(is_new, knew_ref[...], kpage_ref[...])
    v = jnp.where(is_new, vnew_ref[...], vpage_ref[...])
    sel, run = _select_chunk(keys_ref[p], tau_scr[:, :1], bud_scr[:, :1], run_scr[:, :1], _prefix_tri())
    run_scr[...] = jnp.broadcast_to(run, run_scr.shape)
    m = [m_scr[h][:, :1] for h in range(N_HEADS_C)]
    l = [l_scr[h][:, :1] for h in range(N_HEADS_C)]
    acc = [acc_scr[h] for h in range(N_HEADS_C)]
    m, l, acc = _attend_chunk(q_ref[...], k, v, sel, m, l, acc)
    for h in range(N_HEADS_C):
        m_scr[h] = jnp.broadcast_to(m[h], (t, LANE))
        l_scr[h] = jnp.broadcast_to(l[h], (t, LANE))
        acc_scr[h] = acc[h]

    @pl.when(is_new)
    def _():
        o_ref[...] = jnp.concatenate([acc[h] / l[h] for h in range(N_HEADS_C)], axis=1)


def _dsa_sample_attend(page_table, keys, q, cache_k, cache_v, layer, k_new, v_new):
    b, t, _ = q.shape
    n_pages = page_table.shape[1]
    last = n_pages - 1
    n_sel = min(TOPK_MAX, (n_pages * PAGE_SIZE + t) // 4)
    page = pl.BlockSpec((None, None, PAGE_SIZE, W_C), lambda i, p, pt: (layer, pt[i, jnp.minimum(p, last)], 0, 0))
    new = pl.BlockSpec((None, CHUNK, W_C), lambda i, p, pt: (i, 0, 0))
    grid_spec = pltpu.PrefetchScalarGridSpec(
        num_scalar_prefetch=1,
        grid=(b, n_pages + 1),
        in_specs=[pl.BlockSpec((None, n_pages + 1, t, CHUNK), lambda i, p, pt: (i, 0, 0, 0)),
                  pl.BlockSpec((None, t, W_C), lambda i, p, pt: (i, 0, 0)),
                  page, page, new, new],
        out_specs=pl.BlockSpec((None, t, W_C), lambda i, p, pt: (i, 0, 0)),
        scratch_shapes=[pltpu.VMEM((t, LANE), I32), pltpu.VMEM((t, LANE), F32), pltpu.VMEM((t, LANE), F32),
                        pltpu.VMEM((N_HEADS_C, t, LANE), F32), pltpu.VMEM((N_HEADS_C, t, LANE), F32),
                        pltpu.VMEM((N_HEADS_C, t, HEAD_DIM_C), F32)],
    )
    return pl.pallas_call(
        functools.partial(_dsa_sample_attend_kernel, n_pages=n_pages, n_sel=n_sel),
        grid_spec=grid_spec,
        out_shape=jax.ShapeDtypeStruct((b, t, W_C), F32),
        compiler_params=pltpu.CompilerParams(dimension_semantics=("arbitrary", "arbitrary"),
                                             vmem_limit_bytes=VMEM_LIMIT),
        name="dsa_sample_attend",
    )(page_table, keys, q, cache_k, cache_v, k_new, v_new)


PEER_TILE = 256
PEER_ECHUNK = 1024
_N_TOP = PEER_TOPK + 1
_CAND_CELLS = [(a, b) for a in range(_N_TOP) for b in range(_N_TOP) if (a + 1) * (b + 1) <= _N_TOP]
_CAND_ROWS = -(-len(_CAND_CELLS) // 8) * 8


def _top_values(s, n):
    vals, out = s, []
    for _ in range(n):
        m = jnp.max(vals, axis=0, keepdims=True)
        out.append(m)
        vals = jnp.where(vals == m, -jnp.inf, vals)
    return out


def _peer_kernel(x_ref, yabd_ref, yc_ref, wo_abd_ref, wo_c_ref, g_ref, wq_t_ref, k1_ref, k2_ref,
                 u_ref, v_t_ref, gfin_ref, *rest, final):
    if final:
        out_ref, yfin_ref = rest[:2]
        rest = rest[2:]
    else:
        out_ref = rest[0]
        rest = rest[1:]
    h_scr, x_t_scr, q_scr, s2_scr, e2_scr, thr_scr, e1_scr, cand_scr, p_scr, acc_scr = rest
    e = pl.program_id(1)
    n_e = pl.num_programs(1)

    @pl.when(e == 0)
    def _():
        h = (x_ref[...]
             + jnp.dot(yabd_ref[...].astype(BF16), wo_abd_ref[...], preferred_element_type=F32)
             + jnp.dot(yc_ref[...].astype(BF16), wo_c_ref[...], preferred_element_type=F32))
        h_scr[...] = h
        x_t = _rms(h, g_ref[...]).T.astype(BF16)
        x_t_scr[...] = x_t
        q_scr[...] = jnp.dot(wq_t_ref[...], x_t, preferred_element_type=F32)
        acc_scr[...] = jnp.zeros_like(acc_scr)
        cand_scr[...] = jnp.full(cand_scr.shape, -jnp.inf, F32)

        def head_body(hd, carry):
            base = pl.multiple_of(hd * PEER_QDIM, PEER_QDIM)
            s1 = jnp.dot(k1_ref[...], q_scr[pl.ds(base, PEER_HALF), :],
                         preferred_element_type=F32, precision=HIGHEST)
            s2 = jnp.dot(k2_ref[...], q_scr[pl.ds(base + PEER_HALF, PEER_HALF), :],
                         preferred_element_type=F32, precision=HIGHEST)
            t1 = _top_values(s1, _N_TOP)
            t2 = _top_values(s2, _N_TOP)
            for r, (a, b) in enumerate(_CAND_CELLS):
                cand_scr[r:r + 1, :] = t1[a] + t2[b]
            cand = cand_scr[...]
            tops = _top_values(cand, _N_TOP)
            tau = 0.5 * (tops[PEER_TOPK - 1] + tops[PEER_TOPK])
            mx = t1[0] + t2[0]
            z = jnp.sum(jnp.where(cand >= tau, jnp.exp(cand - mx), 0.0), axis=0, keepdims=True)
            s2_scr[hd] = s2
            e2_scr[hd] = jnp.exp(s2 - t2[0])
            thr_scr[hd] = tau - s1
            e1_scr[hd] = jnp.exp(s1 - t1[0]) / z
            return carry

        lax.fori_loop(0, PEER_HEADS, head_body, 0)

    act = _gelu_tanh(jnp.dot(u_ref[...], x_t_scr[...], preferred_element_type=F32))
    rows_per_step = PEER_ECHUNK // PEER_NKEYS
    for gi in range(rows_per_step):
        i1 = e * rows_per_step + gi
        w = None
        for hd in range(PEER_HEADS):
            thr_row = thr_scr[hd, pl.ds(i1, 1), :]
            e1_row = e1_scr[hd, pl.ds(i1, 1), :]
            term = jnp.where(s2_scr[hd] >= thr_row, e2_scr[hd], 0.0) * e1_row
            w = term if w is None else w + term
        rs = slice(gi * PEER_NKEYS, (gi + 1) * PEER_NKEYS)
        p_scr[rs, :] = (w * act[rs, :]).astype(BF16)
    acc_scr[...] += jnp.dot(v_t_ref[...], p_scr[...], preferred_element_type=F32)

    @pl.when(e == n_e - 1)
    def _():
        out = h_scr[...] + acc_scr[...].T
        out_ref[...] = out
        if final:
            yfin_ref[...] = _rms(out, gfin_ref[...])


def _peer(x, yabd, yc, wo_abd, wo_c, g, wq_t, k1, k2, u, v_t, gfin, *, final):
    n = x.shape[0]
    t = PEER_TILE
    n_exp = u.shape[0]
    row = lambda w: pl.BlockSpec((t, w), lambda i, e: (i, 0))
    full = lambda a: pl.BlockSpec(a.shape, lambda i, e: (0, 0))
    n_out = 2 if final else 1
    return pl.pallas_call(
        functools.partial(_peer_kernel, final=final),
        grid=(n // t, n_exp // PEER_ECHUNK),
        in_specs=[row(D_MODEL), row(W_A + W_B + W_D), row(W_C), full(wo_abd), full(wo_c), full(g),
                  full(wq_t), full(k1), full(k2),
                  pl.BlockSpec((PEER_ECHUNK, D_MODEL), lambda i, e: (e, 0)),
                  pl.BlockSpec((D_MODEL, PEER_ECHUNK), lambda i, e: (0, e)),
                  full(gfin)],
        out_specs=[row(D_MODEL)] * n_out,
        out_shape=[jax.ShapeDtypeStruct((n, D_MODEL), F32)] * n_out,
        scratch_shapes=[pltpu.VMEM((t, D_MODEL), F32), pltpu.VMEM((D_MODEL, t), BF16),
                        pltpu.VMEM((PEER_HEADS * PEER_QDIM, t), F32),
                        pltpu.VMEM((PEER_HEADS, PEER_NKEYS, t), F32), pltpu.VMEM((PEER_HEADS, PEER_NKEYS, t), F32),
                        pltpu.VMEM((PEER_HEADS, PEER_NKEYS, t), F32), pltpu.VMEM((PEER_HEADS, PEER_NKEYS, t), F32),
                        pltpu.VMEM((_CAND_ROWS, t), F32), pltpu.VMEM((PEER_ECHUNK, t), BF16),
                        pltpu.VMEM((D_MODEL, t), F32)],
        compiler_params=pltpu.CompilerParams(dimension_semantics=("arbitrary", "arbitrary"),
                                             vmem_limit_bytes=VMEM_LIMIT),
        name="peer",
    )(x, yabd, yc, wo_abd, wo_c, g, wq_t, k1, k2, u, v_t, gfin)


def _history(state, width):
    return jnp.pad(state, ((0, 0), (HALO - state.shape[1], 0), (0, 0)))


def _mixer_params(l, conv_a_w, pool_w, pool_scale, ssm_conv_w, ssm_conv_b, ssm_dt_bias, ssm_a_log, ssm_d, ssm_norm):
    pad_heads = lambda a: jnp.pad(a, (0, LANE - SSM_HEADS))[None, :]
    pw = jax.scipy.linalg.block_diag(*[pool_w[l, g] for g in range(POOL_GROUPS)])
    expand = jnp.repeat(jnp.eye(LANE, SSM_HEADS, dtype=F32), SSM_HEAD_DIM, axis=1)
    return (conv_a_w[l], pw, pool_scale[l][None, :], ssm_conv_w[l], ssm_conv_b[l][None, :],
            pad_heads(ssm_dt_bias[l]), pad_heads(ssm_a_log[l]),
            jnp.repeat(ssm_d[l], SSM_HEAD_DIM)[None, :], ssm_norm[l][None, :], expand)


def kernel(x_prompt, x_sample, state_conv_a, state_pool, cache_k, cache_v, cache_kidx, page_table, state_conv_d, state_ssm, norm_mix, w_in, conv_a_w, pool_w, pool_scale, ssm_conv_w, ssm_conv_b, ssm_dt_bias, ssm_a_log, ssm_d, ssm_norm, w_out, norm_ffn, peer_wq, peer_k1, peer_k2, peer_u, peer_v, norm_final):
    bp, lp, d = x_prompt.shape
    bs, ls, _ = x_sample.shape
    depth = w_in.shape[0]
    n_pages = page_table.shape[1]
    past_len = n_pages * PAGE_SIZE
    n_pool = cache_k.shape[1]

    tabs_p = _rot_tables(jnp.arange(lp, dtype=I32))
    tabs_s = _rot_tables(jnp.tile(past_len + jnp.arange(ls, dtype=I32), bs))
    cache_k2 = cache_k.reshape(depth, n_pool, PAGE_SIZE, W_C)
    cache_v2 = cache_v.reshape(depth, n_pool, PAGE_SIZE, W_C)
    gfin = norm_final[None, :]

    hp = x_prompt.reshape(bp * lp, d)
    hs = x_sample.reshape(bs * ls, d)
    yp = ys = None
    new_p = [[] for _ in range(7)]
    new_s = [[] for _ in range(7)]
    for l in range(depth):
        final = l == depth - 1
        w_packed = _pack_w_in(w_in[l])
        g_mix = norm_mix[l][None, :]
        prm = _mixer_params(l, conv_a_w, pool_w, pool_scale, ssm_conv_w, ssm_conv_b, ssm_dt_bias,
                            ssm_a_log, ssm_d, ssm_norm)
        wo = w_out[l].astype(BF16)
        wo_abd = jnp.concatenate([wo[:W_A + W_B], wo[W_A + W_B + W_C:]], axis=0)
        wo_c = wo[W_A + W_B:W_A + W_B + W_C]
        peer_w = (wo_abd, wo_c, norm_ffn[l][None, :], peer_wq[l].T.astype(BF16), peer_k1[l], peer_k2[l],
                  peer_u[l].astype(BF16), peer_v[l].T.astype(BF16), gfin)

        oa, ob, q, k, v, qi, ki, wi, z, xbc, dt = [
            a.reshape(bp, lp, a.shape[-1]) for a in _inproj(hp, g_mix, w_packed, tabs_p, 256)]
        zeros = lambda r, w: jnp.zeros((bp, r, w), F32)
        y_abd, ua, h_new = _mixer(oa, ob, z, xbc, dt, zeros(HALO, W_A), zeros(HALO, W_B), zeros(HALO, XBC_DIM),
                                  zeros(W_D, SSM_STATE), prm, qin=CHUNK, pos0=0)
        y_c = _dsa_prompt(qi, wi, q, ki, k, v)
        outs = _peer(hp, y_abd.reshape(bp * lp, -1), y_c.reshape(bp * lp, -1), *peer_w, final=final)
        hp = outs[0]
        if final:
            yp = outs[1]
        st = (ua[:, 8 - (CONV_A - 1):], ob[:, lp - POOL_BUF:],
              k.reshape(bp, lp, N_HEADS_C, HEAD_DIM_C), v.reshape(bp, lp, N_HEADS_C, HEAD_DIM_C), ki,
              xbc[:, lp - (SSM_CONV - 1):], h_new.reshape(bp, SSM_HEADS, SSM_HEAD_DIM, SSM_STATE))
        for i in range(7):
            new_p[i].append(st[i])

        oa, ob, q, k, v, qi, ki, wi, z, xbc, dt = [
            a.reshape(bs, ls, a.shape[-1]) for a in _inproj(hs, g_mix, w_packed, tabs_s, bs * ls)]
        y_abd, ua, h_new = _mixer(oa, ob, z, xbc, dt, _history(state_conv_a[l], W_A), _history(state_pool[l], W_B),
                                  _history(state_conv_d[l], XBC_DIM),
                                  state_ssm[l].reshape(bs, W_D, SSM_STATE), prm, qin=ls, pos0=past_len)
        pad_new = lambda a: jnp.pad(a, ((0, 0), (0, CHUNK - ls), (0, 0)))
        keys = _dsa_sample_scores(page_table, qi, wi, cache_kidx, l, pad_new(ki))
        y_c = _dsa_sample_attend(page_table, keys, q, cache_k2, cache_v2, l, pad_new(k), pad_new(v))
        outs = _peer(hs, y_abd.reshape(bs * ls, -1), y_c.reshape(bs * ls, -1), *peer_w, final=final)
        hs = outs[0]
        if final:
            ys = outs[1]
        st = (ua[:, 8 - (CONV_A - 1):], jnp.concatenate([state_pool[l], ob], axis=1)[:, -POOL_BUF:],
              k.reshape(bs, ls, N_HEADS_C, HEAD_DIM_C), v.reshape(bs, ls, N_HEADS_C, HEAD_DIM_C), ki,
              jnp.concatenate([state_conv_d[l], xbc], axis=1)[:, -(SSM_CONV - 1):],
              h_new.reshape(bs, SSM_HEADS, SSM_HEAD_DIM, SSM_STATE))
        for i in range(7):
            new_s[i].append(st[i])

    p_states = [jnp.stack(a, axis=0) for a in new_p]
    s_states = [jnp.stack(a, axis=0) for a in new_s]
    return (yp.reshape(bp, lp, d), ys.reshape(bs, ls, d), *p_states, *s_states)
```
